```python
import math
import jax, jax.numpy as jnp
from jax import lax
import numpy as np

D_MODEL = 1024
BATCH = 16
SEQ = 4096
DEPTH = 1
DEC_BATCH = 4
DEC_SEQ = 8192
PAST_LEN = 128

GRID_W = 64
N_Q_HEADS = 8
N_KV_HEADS = 2
HEAD_DIM = 64
Q_REP = N_Q_HEADS // N_KV_HEADS
Q_BLOCK = 128
ROPE_THETA = 10000.0
SSM_HEADS = 8
SSM_HEAD_DIM = 64
SSM_INNER = SSM_HEADS * SSM_HEAD_DIM
SSM_GROUPS = 2
HEADS_PER_GROUP = SSM_HEADS // SSM_GROUPS
D_STATE = 128
D_CONV = 5
CHUNK = 128
ATTN_DIM = N_Q_HEADS * HEAD_DIM
KV_DIM = N_KV_HEADS * HEAD_DIM
BC_DIM = SSM_GROUPS * D_STATE
CONV_DIM = SSM_INNER + 2 * BC_DIM
MIX_DIM = ATTN_DIM + SSM_INNER
IN_DIM = ATTN_DIM + 2 * KV_DIM + SSM_INNER + CONV_DIM + 2 * SSM_HEADS
D_FF = ((8 * D_MODEL + 3 * 256 - 1) // (3 * 256)) * 256
N_MOD = 6
EPS = 1e-6

kernel_name = "hymba_attn_ssd_adaln_encoder"


def rms_norm(x, g):
    xf = x.astype(jnp.float32)
    y = xf * lax.rsqrt(jnp.mean(xf * xf, axis=-1, keepdims=True) + EPS)
    return y.astype(x.dtype) * g


def _rope_1d(xa, pos):
    half = xa.shape[-1] // 2
    freqs = ROPE_THETA ** (-jnp.arange(half, dtype=jnp.float32) / half)
    ang = pos.astype(jnp.float32)[:, None] * freqs[None, :]
    cos = jnp.cos(ang)[None, :, None, :].astype(xa.dtype)
    sin = jnp.sin(ang)[None, :, None, :].astype(xa.dtype)
    x1, x2 = xa[..., :half], xa[..., half:]
    return jnp.concatenate([x1 * cos - x2 * sin, x1 * sin + x2 * cos], axis=-1)


def axial_rope(x, rows, cols):
    ra = HEAD_DIM // 2
    return jnp.concatenate([_rope_1d(x[..., :ra], rows), _rope_1d(x[..., ra:], cols)], axis=-1)


def block_attention(q, k, v):
    b, s = q.shape[0], q.shape[1]
    nb = s // Q_BLOCK
    scale = 1.0 / math.sqrt(HEAD_DIM)
    qb = q.reshape(b, nb, Q_BLOCK, N_KV_HEADS, Q_REP, HEAD_DIM).transpose(1, 0, 2, 3, 4, 5)

    def one_block(qblk):
        scores = jnp.einsum("bqgrd,bkgd->bgrqk", qblk, k, preferred_element_type=jnp.float32) * scale
        p = jax.nn.softmax(scores, axis=-1).astype(v.dtype)
        return jnp.einsum("bgrqk,bkgd->bqgrd", p, v)

    out = lax.map(one_block, qb)
    return out.transpose(1, 0, 2, 3, 4, 5).reshape(b, s, ATTN_DIM)


def segsum(a):
    cs = jnp.cumsum(a, axis=-1)
    diff = cs[..., :, None] - cs[..., None, :]
    t = a.shape[-1]
    mask = jnp.tril(jnp.ones((t, t), dtype=bool))
    return jnp.where(mask, diff, -jnp.inf)


def ssd_scan(x, dt, a, bm, cm):
    b, s = x.shape[0], x.shape[1]
    nc = s // CHUNK
    dtype = x.dtype
    xdt = (x.astype(jnp.float32) * dt[..., None]).astype(dtype)
    xdt = xdt.reshape(b, nc, CHUNK, SSM_HEADS, SSM_HEAD_DIM)
    bc = bm.reshape(b, nc, CHUNK, SSM_HEADS, D_STATE)
    cc = cm.reshape(b, nc, CHUNK, SSM_HEADS, D_STATE)
    a_dt = (dt * a.astype(jnp.float32)).reshape(b, nc, CHUNK, SSM_HEADS).transpose(0, 3, 1, 2)
    a_cs = jnp.cumsum(a_dt, axis=-1)
    decay_in = jnp.exp(segsum(a_dt)).astype(dtype)
    cb = jnp.einsum("bclhn,bcshn->bhcls", cc, bc)
    y_diag = jnp.einsum("bhcls,bcshp->bclhp", cb * decay_in, xdt)
    decay_states = jnp.exp(a_cs[..., -1:] - a_cs).astype(dtype)
    states = jnp.einsum("bclhn,bhcl,bclhp->bchpn", bc, decay_states, xdt)
    states = jnp.concatenate([jnp.zeros_like(states[:, :1]), states], axis=1)
    chunk_a = jnp.pad(a_cs[..., -1], ((0, 0), (0, 0), (1, 0)))
    decay_chunk = jnp.exp(segsum(chunk_a)).astype(dtype)
    states = jnp.einsum("bhzc,bchpn->bzhpn", decay_chunk, states)[:, :-1]
    decay_out = jnp.exp(a_cs).astype(dtype)
    y_off = jnp.einsum("bclhn,bchpn,bhcl->bclhp", cc, states, decay_out)
    return (y_diag + y_off).reshape(b, s, SSM_HEADS, SSM_HEAD_DIM)


def encoder_layer(x, c, w_mod, b_mod, norm_mix_g, w_in, q_norm_g, k_norm_g, conv_w, conv_b,
                  dt_bias_fwd, a_log_fwd, dt_bias_bwd, a_log_bwd, d_skip, ssd_norm_g, w_out,
                  norm_ffn_g, w_gate_up, w_down):
    b, s, _ = x.shape
    n_rows = s // GRID_W
    rows = jnp.repeat(jnp.arange(n_rows), GRID_W)
    cols = jnp.tile(jnp.arange(GRID_W), n_rows)

    mod = jax.nn.silu(c) @ w_mod + b_mod
    shift1, scale1, gate1, shift2, scale2, gate2 = [m[:, None, :] for m in jnp.split(mod, N_MOD, axis=-1)]

    h = rms_norm(x, norm_mix_g) * (1 + scale1) + shift1
    proj = h @ w_in
    sizes = [ATTN_DIM, KV_DIM, KV_DIM, SSM_INNER, CONV_DIM, SSM_HEADS, SSM_HEADS]
    parts, off = [], 0
    for sz in sizes:
        parts.append(proj[..., off:off + sz])
        off += sz
    q, k, v, z, xbc, dt_f, dt_b = parts

    q = rms_norm(q.reshape(b, s, N_Q_HEADS, HEAD_DIM), q_norm_g)
    k = rms_norm(k.reshape(b, s, N_KV_HEADS, HEAD_DIM), k_norm_g)
    v = v.reshape(b, s, N_KV_HEADS, HEAD_DIM)
    q = axial_rope(q, rows, cols)
    k = axial_rope(k, rows, cols)
    attn_out = block_attention(q, k, v)

    pad = D_CONV // 2
    xbc = lax.conv_general_dilated(xbc, conv_w[:, None, :], window_strides=(1,), padding=[(pad, pad)],
                                   dimension_numbers=("NWC", "WIO", "NWC"), feature_group_count=CONV_DIM)
    xbc = jax.nn.silu(xbc + conv_b)
    xs = xbc[..., :SSM_INNER].reshape(b, s, SSM_HEADS, SSM_HEAD_DIM)
    bm = jnp.repeat(xbc[..., SSM_INNER:SSM_INNER + BC_DIM].reshape(b, s, SSM_GROUPS, D_STATE), HEADS_PER_GROUP, axis=2)
    cm = jnp.repeat(xbc[..., SSM_INNER + BC_DIM:].reshape(b, s, SSM_GROUPS, D_STATE), HEADS_PER_GROUP, axis=2)
    dtf = jax.nn.softplus(dt_f.astype(jnp.float32) + dt_bias_fwd.astype(jnp.float32))
    dtb = jax.nn.softplus(dt_b.astype(jnp.float32) + dt_bias_bwd.astype(jnp.float32))
    a_f = -jnp.exp(a_log_fwd.astype(jnp.float32))
    a_b = -jnp.exp(a_log_bwd.astype(jnp.float32))
    y_fwd = ssd_scan(xs, dtf, a_f, bm, cm)
    y_bwd = ssd_scan(xs[:, ::-1], dtb[:, ::-1], a_b, bm[:, ::-1], cm[:, ::-1])[:, ::-1]
    y_ssd = (y_fwd + y_bwd + d_skip[:, None] * xs).reshape(b, s, SSM_INNER)
    ssd_out = rms_norm(y_ssd * jax.nn.silu(z), ssd_norm_g)

    mix = jnp.concatenate([attn_out, ssd_out], axis=-1) @ w_out
    x = x + gate1 * mix

    h2 = rms_norm(x, norm_ffn_g) * (1 + scale2) + shift2
    gu = h2 @ w_gate_up
    ffn = (jax.nn.silu(gu[..., :D_FF]) * gu[..., D_FF:]) @ w_down
    return x + gate2 * ffn


def setup_inputs(seed: int = 0) -> dict:
    key = jax.random.key(seed)
    ks = jax.random.split(key, 24)
    f32 = jnp.float32
    D = D_MODEL

    def nrm(k, shape, scale):
        return jax.random.normal(k, shape, f32) * scale

    dt0_f = jnp.exp(jax.random.uniform(ks[10], (DEPTH, SSM_HEADS), f32, math.log(1e-3), math.log(1e-1)))
    dt0_b = jnp.exp(jax.random.uniform(ks[11], (DEPTH, SSM_HEADS), f32, math.log(1e-3), math.log(1e-1)))
    return {
        "x_prompt": nrm(ks[0], (BATCH, SEQ, D), 1.0),
        "x_sample": nrm(ks[1], (DEC_BATCH, DEC_SEQ, D), 1.0),
        "c_prompt": nrm(ks[2], (BATCH, D), 1.0),
        "c_sample": nrm(ks[3], (DEC_BATCH, D), 1.0),
        "w_mod": nrm(ks[4], (DEPTH, D, N_MOD * D), 0.5 * D ** -0.5),
        "b_mod": nrm(ks[5], (DEPTH, N_MOD * D), 0.02),
        "norm_mix_g": 1.0 + nrm(ks[6], (DEPTH, D), 0.02),
        "w_in": nrm(ks[7], (DEPTH, D, IN_DIM), D ** -0.5),
        "q_norm_g": 1.0 + nrm(ks[8], (DEPTH, HEAD_DIM), 0.02),
        "k_norm_g": 1.0 + nrm(ks[9], (DEPTH, HEAD_DIM), 0.02),
        "conv_w": nrm(ks[12], (DEPTH, D_CONV, CONV_DIM), D_CONV ** -0.5),
        "conv_b": nrm(ks[13], (DEPTH, CONV_DIM), 0.02),
        "dt_bias_fwd": dt0_f + jnp.log(-jnp.expm1(-dt0_f)),
        "a_log_fwd": jnp.log(jax.random.uniform(ks[14], (DEPTH, SSM_HEADS), f32, 1.0, 16.0)),
        "dt_bias_bwd": dt0_b + jnp.log(-jnp.expm1(-dt0_b)),
        "a_log_bwd": jnp.log(jax.random.uniform(ks[15], (DEPTH, SSM_HEADS), f32, 1.0, 16.0)),
        "d_skip": 1.0 + nrm(ks[16], (DEPTH, SSM_HEADS), 0.1),
        "ssd_norm_g": 1.0 + nrm(ks[17], (DEPTH, SSM_INNER), 0.02),
        "w_out": nrm(ks[18], (DEPTH, MIX_DIM, D), MIX_DIM ** -0.5),
        "norm_ffn_g": 1.0 + nrm(ks[19], (DEPTH, D), 0.02),
        "w_gate_up": nrm(ks[20], (DEPTH, D, 2 * D_FF), D ** -0.5),
        "w_down": nrm(ks[21], (DEPTH, D_FF, D), D_FF ** -0.5),
    }


def reference(x_prompt, x_sample, c_prompt, c_sample, w_mod, b_mod, norm_mix_g, w_in, q_norm_g,
              k_norm_g, conv_w, conv_b, dt_bias_fwd, a_log_fwd, dt_bias_bwd, a_log_bwd, d_skip,
              ssd_norm_g, w_out, norm_ffn_g, w_gate_up, w_down):
    def run(x, c):
        for l in range(DEPTH):
            x = encoder_layer(x, c, w_mod[l], b_mod[l], norm_mix_g[l], w_in[l], q_norm_g[l], k_norm_g[l],
                              conv_w[l], conv_b[l], dt_bias_fwd[l], a_log_fwd[l], dt_bias_bwd[l],
                              a_log_bwd[l], d_skip[l], ssd_norm_g[l], w_out[l], norm_ffn_g[l],
                              w_gate_up[l], w_down[l])
        return x

    y_prompt = run(x_prompt, c_prompt)
    y_sample = run(x_sample, c_sample)
    return (y_prompt, y_sample)
```

```python
import functools
import math

import jax
import jax.numpy as jnp
from jax import lax
from jax.experimental import pallas as pl
from jax.experimental.pallas import tpu as pltpu

F32 = jnp.float32
BF16 = jnp.bfloat16

D_MODEL = 1024
GRID_W = 64
N_Q_HEADS = 8
N_KV_HEADS = 2
HEAD_DIM = 64
Q_REP = N_Q_HEADS // N_KV_HEADS
ROPE_THETA = 10000.0
SSM_HEADS = 8
SSM_HEAD_DIM = 64
SSM_INNER = SSM_HEADS * SSM_HEAD_DIM
SSM_GROUPS = 2
HEADS_PER_GROUP = SSM_HEADS // SSM_GROUPS
D_STATE = 128
D_CONV = 5
CHUNK = 128
ATTN_DIM = N_Q_HEADS * HEAD_DIM
KV_DIM = N_KV_HEADS * HEAD_DIM
BC_DIM = SSM_GROUPS * D_STATE
CONV_DIM = SSM_INNER + 2 * BC_DIM
D_FF = ((8 * D_MODEL + 3 * 256 - 1) // (3 * 256)) * 256
N_MOD = 6
EPS = 1e-6

LANES = 128
HALO = 16
GROUP_W = HEADS_PER_GROUP * SSM_HEAD_DIM
VMEM_LIMIT = 48 * 1024 * 1024


def _dot(a, b):
    return jnp.dot(a, b, preferred_element_type=F32)


def _split3(x):
    hi = x.astype(BF16)
    r = x - hi.astype(F32)
    mid = r.astype(BF16)
    lo = (r - mid.astype(F32)).astype(BF16)
    return hi, mid, lo


def _dot_exact_rhs(sel, x):
    hi, mid, lo = _split3(x)
    return _dot(sel, hi) + _dot(sel, mid) + _dot(sel, lo)


def _dot_exact_lhs(x, sel):
    hi, mid, lo = _split3(x)
    return _dot(hi, sel) + _dot(mid, sel) + _dot(lo, sel)


def _silu(x):
    return x * jax.nn.sigmoid(x)


def _mod_kernel(c_ref, w_ref, b_ref, o_ref):
    c = c_ref[...]
    o_ref[...] = _dot(_silu(c), w_ref[...]) + b_ref[...]


def _mod(c_all, w_mod, b_mod):
    nb = c_all.shape[0]
    tn = 512
    return pl.pallas_call(
        _mod_kernel,
        grid=(N_MOD * D_MODEL // tn,),
        in_specs=[
            pl.BlockSpec((nb, D_MODEL), lambda j: (0, 0)),
            pl.BlockSpec((D_MODEL, tn), lambda j: (0, j)),
            pl.BlockSpec((1, tn), lambda j: (0, j)),
        ],
        out_specs=pl.BlockSpec((nb, tn), lambda j: (0, j)),
        out_shape=jax.ShapeDtypeStruct((nb, N_MOD * D_MODEL), F32),
        compiler_params=pltpu.CompilerParams(dimension_semantics=("parallel",)),
        name="mod",
    )(c_all, w_mod, b_mod.reshape(1, -1))


def _inproj_kernel(x_ref, mod_ref, g_ref, wq_ref, wk_ref, wv_ref, wz_ref, wx_ref, wdt_ref,
                   qg_ref, kg_ref, sq_ref, sk_ref, cos_ref, sina_ref, sinb_ref,
                   q_ref, kt_ref, v_ref, z_ref, xbc_ref, dt_ref):
    x = x_ref[0]
    ms = jnp.mean(x * x, axis=-1, keepdims=True)
    shift1 = mod_ref[0, 0:1, :]
    scale1 = mod_ref[0, 1:2, :]
    h = (x * lax.rsqrt(ms + EPS)) * g_ref[...] * (1.0 + scale1) + shift1
    hb = h.astype(BF16)

    cos = cos_ref[...]
    sina = sina_ref[...]
    sinb = sinb_ref[...]

    def head_norm(t, s_ref, gain):
        t2 = t * t
        hi = t2.astype(BF16)
        lo = (t2 - hi.astype(F32)).astype(BF16)
        msq = _dot(hi, s_ref[...]) + _dot(lo, s_ref[...])
        return t * lax.rsqrt(msq + EPS) * gain

    def rope(t):
        return t * cos + pltpu.roll(t, LANES - 16, 1) * sina + pltpu.roll(t, 16, 1) * sinb

    q = head_norm(_dot(hb, wq_ref[...]), sq_ref, qg_ref[...])
    scale = 1.0 / math.sqrt(HEAD_DIM)
    for j in range(ATTN_DIM // LANES):
        qj = rope(q[:, j * LANES:(j + 1) * LANES]) * scale
        q_ref[0, :, j * LANES:(j + 1) * LANES] = qj.astype(BF16)

    k = rope(head_norm(_dot(hb, wk_ref[...]), sk_ref, kg_ref[...]))
    kt_ref[0] = k.T.astype(BF16)

    v = _dot(hb, wv_ref[...])
    lane = lax.broadcasted_iota(jnp.int32, v.shape, 1)
    ones_col = jnp.where(lane == HEAD_DIM, 1.0, 0.0)
    v_ref[0, 0] = jnp.where(lane < HEAD_DIM, v, ones_col).astype(BF16)
    v_ref[0, 1] = jnp.where(lane < HEAD_DIM, pltpu.roll(v, HEAD_DIM, 1), ones_col).astype(BF16)

    z_ref[0] = _dot(hb, wz_ref[...]).astype(BF16)
    xbc_ref[0] = _dot(hb, wx_ref[...]).astype(BF16)
    dt_ref[0] = _dot(hb, wdt_ref[...])


def _inproj(x, mod3, boff, consts, tm=512):
    b, s, _ = x.shape
    (g, wq, wk, wv, wz, wx, wdt, qg, kg, sq, sk, cos, sina, sinb) = consts
    full = lambda a: pl.BlockSpec(a.shape, lambda bi, si: (0,) * a.ndim)
    tab = pl.BlockSpec((tm, LANES), lambda bi, si: (si, 0))
    return pl.pallas_call(
        _inproj_kernel,
        grid=(b, s // tm),
        in_specs=[
            pl.BlockSpec((1, tm, D_MODEL), lambda bi, si: (bi, si, 0)),
            pl.BlockSpec((1, N_MOD, D_MODEL), lambda bi, si: (bi + boff, 0, 0)),
            full(g), full(wq), full(wk), full(wv), full(wz), full(wx), full(wdt),
            full(qg), full(kg), full(sq), full(sk), tab, tab, tab,
        ],
        out_specs=[
            pl.BlockSpec((1, tm, ATTN_DIM), lambda bi, si: (bi, si, 0)),
            pl.BlockSpec((1, KV_DIM, tm), lambda bi, si: (bi, 0, si)),
            pl.BlockSpec((1, N_KV_HEADS, tm, LANES), lambda bi, si: (bi, 0, si, 0)),
            pl.BlockSpec((1, tm, SSM_INNER), lambda bi, si: (bi, si, 0)),
            pl.BlockSpec((1, tm, CONV_DIM), lambda bi, si: (bi, si, 0)),
            pl.BlockSpec((1, tm, LANES), lambda bi, si: (bi, si, 0)),
        ],
        out_shape=[
            jax.ShapeDtypeStruct((b, s, ATTN_DIM), BF16),
            jax.ShapeDtypeStruct((b, KV_DIM, s), BF16),
            jax.ShapeDtypeStruct((b, N_KV_HEADS, s, LANES), BF16),
            jax.ShapeDtypeStruct((b, s, SSM_INNER), BF16),
            jax.ShapeDtypeStruct((b, s, CONV_DIM), BF16),
            jax.ShapeDtypeStruct((b, s, LANES), F32),
        ],
        compiler_params=pltpu.CompilerParams(
            dimension_semantics=("parallel", "parallel"), vmem_limit_bytes=VMEM_LIMIT),
        name="inproj",
    )(x, mod3, g, wq, wk, wv, wz, wx, wdt, qg, kg, sq, sk, cos, sina, sinb)


def _attn_kernel(q_ref, kt_ref, v_ref, o_ref, q_scr, m_scr, acc_scr, *, tq, tk):
    ki = pl.program_id(3)

    @pl.when(ki == 0)
    def _():
        for r in range(Q_REP):
            q_scr[r * tq:(r + 1) * tq, :] = q_ref[0, :, r * HEAD_DIM:(r + 1) * HEAD_DIM]
        m_scr[...] = jnp.full(m_scr.shape, -jnp.inf, F32)
        acc_scr[...] = jnp.zeros(acc_scr.shape, F32)

    s = _dot(q_scr[...], kt_ref[0])
    m_prev = m_scr[...]
    m_new = jnp.maximum(m_prev, jnp.max(s, axis=-1, keepdims=True))
    alpha = jnp.exp(m_prev - m_new)
    p = jnp.concatenate(
        [jnp.exp(s[:, j * LANES:(j + 1) * LANES] - m_new) for j in range(tk // LANES)], axis=1)
    acc_scr[...] = acc_scr[...] * alpha + _dot(p.astype(BF16), v_ref[0, 0])
    m_scr[...] = m_new

    @pl.when(ki == pl.num_programs(3) - 1)
    def _():
        acc = acc_scr[...]
        o = acc[:, :HEAD_DIM] / acc[:, HEAD_DIM:HEAD_DIM + 1]
        for r in range(Q_REP):
            o_ref[0, :, r * HEAD_DIM:(r + 1) * HEAD_DIM] = o[r * tq:(r + 1) * tq].astype(BF16)


def _attn(q, kt, vext, tq=256, tk=512):
    b, s, _ = q.shape
    gw = Q_REP * HEAD_DIM
    return pl.pallas_call(
        functools.partial(_attn_kernel, tq=tq, tk=tk),
        grid=(b, N_KV_HEADS, s // tq, s // tk),
        in_specs=[
            pl.BlockSpec((1, tq, gw), lambda bi, gi, qi, ki: (bi, qi, gi)),
            pl.BlockSpec((1, HEAD_DIM, tk), lambda bi, gi, qi, ki: (bi, gi, ki)),
            pl.BlockSpec((1, 1, tk, LANES), lambda bi, gi, qi, ki: (bi, gi, ki, 0)),
        ],
        out_specs=pl.BlockSpec((1, tq, gw), lambda bi, gi, qi, ki: (bi, qi, gi)),
        out_shape=jax.ShapeDtypeStruct((b, s, ATTN_DIM), BF16),
        scratch_shapes=[
            pltpu.VMEM((Q_REP * tq, HEAD_DIM), BF16),
            pltpu.VMEM((Q_REP * tq, LANES), F32),
            pltpu.VMEM((Q_REP * tq, LANES), F32),
        ],
        compiler_params=pltpu.CompilerParams(
            dimension_semantics=("parallel", "parallel", "parallel", "arbitrary"),
            vmem_limit_bytes=VMEM_LIMIT),
        name="attn",
    )(q, kt, vext)


def _ssd_kernel(*refs, nc, reverse):
    if reverse:
        (xbc_ref, prev_ref, next_ref, dt_ref, cw_ref, cb_ref, dtb_ref, alog_ref, ex_ref,
         yf_ref, z_ref, dskip_ref, ng_ref, o_ref, h_scr, xe_scr) = refs
    else:
        (xbc_ref, prev_ref, next_ref, dt_ref, cw_ref, cb_ref, dtb_ref, alog_ref, ex_ref,
         o_ref, h_scr, xe_scr) = refs
    step = pl.program_id(1)
    chunk = (nc - 1 - step) if reverse else step

    @pl.when(step == 0)
    def _():
        h_scr[...] = jnp.zeros(h_scr.shape, F32)

    prev = prev_ref[0, HALO - 8:HALO, :].astype(F32)
    nxt = next_ref[0, 0:8, :].astype(F32)
    xe_scr[0:8, :] = jnp.where(chunk > 0, prev, 0.0)
    xe_scr[8:8 + CHUNK, :] = xbc_ref[0].astype(F32)
    xe_scr[8 + CHUNK:16 + CHUNK, :] = jnp.where(chunk < nc - 1, nxt, 0.0)
    pad = D_CONV // 2
    acc = jnp.broadcast_to(cb_ref[...], (CHUNK, CONV_DIM))
    for k in range(D_CONV):
        acc = acc + xe_scr[8 - pad + k:8 - pad + k + CHUNK, :] * cw_ref[k:k + 1, :]
    act = _silu(acc)
    xs = act[:, :SSM_INNER]

    lane = lax.broadcasted_iota(jnp.int32, (1, LANES), 1)
    a_row = jnp.where(lane < 2 * SSM_HEADS, -jnp.exp(alog_ref[...]), 0.0)
    pre = dt_ref[0] + dtb_ref[...]
    dtp = jnp.maximum(pre, 0.0) + jnp.log1p(jnp.exp(-jnp.abs(pre)))
    a_dt = dtp * a_row

    li = lax.broadcasted_iota(jnp.int32, (CHUNK, CHUNK), 0)
    mi = lax.broadcasted_iota(jnp.int32, (CHUNK, CHUNK), 1)
    keep = (mi >= li) if reverse else (mi <= li)
    tri = jnp.where(keep, 1.0, 0.0).astype(BF16)
    cs = _dot_exact_rhs(tri, a_dt)
    cs_t = cs.T

    ex = ex_ref[...]
    csx = _dot_exact_lhs(cs, ex)
    dtx = _dot_exact_lhs(dtp, ex)
    end = CHUNK - 1 if not reverse else 0
    cs_end = csx[end:end + 1, :]
    xdt = xs * dtx
    xw = (xdt * jnp.exp(cs_end - csx)).astype(BF16)
    dec_out = jnp.exp(csx)
    dec_chunk = jnp.exp(cs_end)
    xdt_b = xdt.astype(BF16)

    base = SSM_HEADS if reverse else 0
    lane_head = lax.broadcasted_iota(jnp.int32, (CHUNK, GROUP_W), 1) // SSM_HEAD_DIM
    ys = []
    for g in range(SSM_GROUPS):
        gs = slice(g * GROUP_W, (g + 1) * GROUP_W)
        bm = act[:, SSM_INNER + g * D_STATE:SSM_INNER + (g + 1) * D_STATE]
        cm = act[:, SSM_INNER + BC_DIM + g * D_STATE:SSM_INNER + BC_DIM + (g + 1) * D_STATE]
        bm_b = bm.astype(BF16)
        cm_b = cm.astype(BF16)
        cb = lax.dot_general(cm_b, bm_b, (((1,), (1,)), ((), ())), preferred_element_type=F32)
        m_parts = []
        rhs_parts = []
        for r in range(HEADS_PER_GROUP):
            j = base + g * HEADS_PER_GROUP + r
            diff = cs[:, j:j + 1] - cs_t[j:j + 1, :]
            dec = jnp.exp(jnp.where(keep, diff, -jnp.inf))
            m_parts.append((cb * dec).astype(BF16))
            rhs_parts.append(jnp.where(lane_head == r, xdt_b[:, gs], jnp.zeros((), BF16)))
        y_diag = _dot(jnp.concatenate(m_parts, axis=1), jnp.concatenate(rhs_parts, axis=0))
        h_old = h_scr[g]
        y_off = _dot(cm_b, h_old.astype(BF16)) * dec_out[:, gs]
        s_new = _dot(bm.T.astype(BF16), xw[:, gs])
        h_scr[g] = h_old * dec_chunk[:, gs] + s_new
        ys.append(y_diag + y_off)
    y = jnp.concatenate(ys, axis=1)

    if not reverse:
        o_ref[0] = y
    else:
        y = yf_ref[0] + y + dskip_ref[...] * xs
        gated = y * _silu(z_ref[0].astype(F32))
        ms = jnp.mean(gated * gated, axis=-1, keepdims=True)
        o_ref[0] = ((gated * lax.rsqrt(ms + EPS)) * ng_ref[...]).astype(BF16)


def _ssd_pass(xbc, dt, consts, reverse, extra=()):
    b, s, _ = xbc.shape
    nc = s // CHUNK
    per = CHUNK // HALO
    last_halo = s // HALO - 1
    cw, cbias, dtb, alog, ex = consts
    pos = (lambda c: nc - 1 - c) if reverse else (lambda c: c)
    full = lambda a: pl.BlockSpec(a.shape, lambda bi, c: (0,) * a.ndim)
    tok = lambda w: pl.BlockSpec((1, CHUNK, w), lambda bi, c: (bi, pos(c), 0))
    in_specs = [
        tok(CONV_DIM),
        pl.BlockSpec((1, HALO, CONV_DIM), lambda bi, c: (bi, jnp.maximum(pos(c) * per - 1, 0), 0)),
        pl.BlockSpec((1, HALO, CONV_DIM),
                     lambda bi, c: (bi, jnp.minimum((pos(c) + 1) * per, last_halo), 0)),
        tok(LANES), full(cw), full(cbias), full(dtb), full(alog), full(ex),
    ]
    args = [xbc, xbc, xbc, dt, cw, cbias, dtb, alog, ex]
    if reverse:
        yf, z, dskip, ng = extra
        in_specs += [tok(SSM_INNER), tok(SSM_INNER), full(dskip), full(ng)]
        args += [yf, z, dskip, ng]
        out_dtype = BF16
    else:
        out_dtype = F32
    return pl.pallas_call(
        functools.partial(_ssd_kernel, nc=nc, reverse=reverse),
        grid=(b, nc),
        in_specs=in_specs,
        out_specs=tok(SSM_INNER),
        out_shape=jax.ShapeDtypeStruct((b, s, SSM_INNER), out_dtype),
        scratch_shapes=[
            pltpu.VMEM((SSM_GROUPS, D_STATE, GROUP_W), F32),
            pltpu.VMEM((CHUNK + 16, CONV_DIM), F32),
        ],
        compiler_params=pltpu.CompilerParams(
            dimension_semantics=("parallel", "arbitrary"), vmem_limit_bytes=VMEM_LIMIT),
        name="ssd_bwd" if reverse else "ssd_fwd",
    )(*args)


def _out_ffn_kernel(x_ref, attn_ref, ssd_ref, mod_ref, wo_a_ref, wo_s_ref, g_ref, wg_ref, wu_ref,
                    wd_ref, o_ref):
    x = x_ref[0]
    gate1 = mod_ref[0, 2:3, :]
    shift2 = mod_ref[0, 3:4, :]
    scale2 = mod_ref[0, 4:5, :]
    gate2 = mod_ref[0, 5:6, :]
    mix = _dot(attn_ref[0], wo_a_ref[...]) + _dot(ssd_ref[0], wo_s_ref[...])
    x1 = x + gate1 * mix
    ms = jnp.mean(x1 * x1, axis=-1, keepdims=True)
    h2 = ((x1 * lax.rsqrt(ms + EPS)) * g_ref[...] * (1.0 + scale2) + shift2).astype(BF16)
    act = (_silu(_dot(h2, wg_ref[...])) * _dot(h2, wu_ref[...])).astype(BF16)
    o_ref[0] = x1 + gate2 * _dot(act, wd_ref[...])


def _out_ffn(x, attn, ssd, mod3, boff, consts, tm=512):
    b, s, _ = x.shape
    wo_a, wo_s, g, wg, wu, wd = consts
    full = lambda a: pl.BlockSpec(a.shape, lambda bi, si: (0,) * a.ndim,
                                  pipeline_mode=pl.Buffered(1))
    tok = lambda w: pl.BlockSpec((1, tm, w), lambda bi, si: (bi, si, 0))
    return pl.pallas_call(
        _out_ffn_kernel,
        grid=(b, s // tm),
        in_specs=[
            tok(D_MODEL), tok(ATTN_DIM), tok(SSM_INNER),
            pl.BlockSpec((1, N_MOD, D_MODEL), lambda bi, si: (bi + boff, 0, 0)),
            full(wo_a), full(wo_s), full(g), full(wg), full(wu), full(wd),
        ],
        out_specs=tok(D_MODEL),
        out_shape=jax.ShapeDtypeStruct((b, s, D_MODEL), F32),
        compiler_params=pltpu.CompilerParams(
            dimension_semantics=("parallel", "parallel"), vmem_limit_bytes=VMEM_LIMIT),
        name="out_ffn",
    )(x, attn, ssd, mod3, wo_a, wo_s, g, wg, wu, wd)


def _rope_tables(s):
    t = jnp.arange(s)
    half = HEAD_DIM // 4
    freqs = ROPE_THETA ** (-jnp.arange(half, dtype=F32) / half)
    ang_r = (t // GRID_W).astype(F32)[:, None] * freqs[None, :]
    ang_c = (t % GRID_W).astype(F32)[:, None] * freqs[None, :]
    ang = jnp.concatenate([ang_r, ang_r, ang_c, ang_c], axis=-1)
    first_half = (jnp.arange(HEAD_DIM) % (2 * half)) < half
    cos = jnp.cos(ang)
    sin = jnp.sin(ang)
    sina = jnp.where(first_half, -sin, 0.0)
    sinb = jnp.where(first_half, 0.0, sin)
    rep = LANES // HEAD_DIM
    return tuple(jnp.tile(a, (1, rep)) for a in (cos, sina, sinb))


def _block_mean(width):
    idx = jnp.arange(width) // HEAD_DIM
    return jnp.where(idx[:, None] == idx[None, :], 1.0 / HEAD_DIM, 0.0).astype(BF16)


def _pad_lanes(row):
    return jnp.pad(row.astype(F32), (0, LANES - row.shape[0])).reshape(1, LANES)


def kernel(x_prompt, x_sample, c_prompt, c_sample, w_mod, b_mod, norm_mix_g, w_in, q_norm_g, k_norm_g, conv_w, conv_b, dt_bias_fwd, a_log_fwd, dt_bias_bwd, a_log_bwd, d_skip, ssd_norm_g, w_out, norm_ffn_g, w_gate_up, w_down):
    assert w_mod.shape[0] == 1
    l = 0
    c_all = jnp.concatenate([c_prompt, c_sample], axis=0)
    mod3 = _mod(c_all, w_mod[l], b_mod[l]).reshape(c_all.shape[0], N_MOD, D_MODEL)

    w = w_in[l].astype(BF16)
    o0 = 0
    o1 = o0 + ATTN_DIM
    o2 = o1 + KV_DIM
    o3 = o2 + KV_DIM
    o4 = o3 + SSM_INNER
    o5 = o4 + CONV_DIM
    wdt = jnp.pad(w[:, o5:], ((0, 0), (0, LANES - 2 * SSM_HEADS)))
    in_consts = (
        norm_mix_g[l].reshape(1, -1), w[:, o0:o1], w[:, o1:o2], w[:, o2:o3], w[:, o3:o4],
        w[:, o4:o5], wdt,
        jnp.tile(q_norm_g[l], N_Q_HEADS).reshape(1, -1),
        jnp.tile(k_norm_g[l], N_KV_HEADS).reshape(1, -1),
        _block_mean(ATTN_DIM), _block_mean(KV_DIM),
    )
    head_of_lane = jnp.arange(SSM_INNER) // SSM_HEAD_DIM
    expand_f = (jnp.arange(LANES)[:, None] == head_of_lane[None, :]).astype(BF16)
    expand_b = (jnp.arange(LANES)[:, None] == head_of_lane[None, :] + SSM_HEADS).astype(BF16)
    dtb = _pad_lanes(jnp.concatenate([dt_bias_fwd[l], dt_bias_bwd[l]]))
    alog = _pad_lanes(jnp.concatenate([a_log_fwd[l], a_log_bwd[l]]))
    ssd_common = (conv_w[l], conv_b[l].reshape(1, -1), dtb, alog)
    dskip_x = jnp.repeat(d_skip[l], SSM_HEAD_DIM).reshape(1, -1)
    ng = ssd_norm_g[l].reshape(1, -1)
    wo = w_out[l].astype(BF16)
    wgu = w_gate_up[l].astype(BF16)
    ffn_consts = (wo[:ATTN_DIM], wo[ATTN_DIM:], norm_ffn_g[l].reshape(1, -1),
                  wgu[:, :D_FF], wgu[:, D_FF:], w_down[l].astype(BF16))

    def run(x, boff):
        s = x.shape[1]
        q, kt, vext, z, xbc, dt = _inproj(x, mod3, boff, in_consts + _rope_tables(s))
        attn = _attn(q, kt, vext)
        yf = _ssd_pass(xbc, dt, ssd_common + (expand_f,), reverse=False)
        ssd = _ssd_pass(xbc, dt, ssd_common + (expand_b,), reverse=True,
                        extra=(yf, z, dskip_x, ng))
        return _out_ffn(x, attn, ssd, mod3, boff, ffn_consts)

    return run(x_prompt, 0), run(x_sample, c_prompt.shape[0])
```

```python
import functools
import math

import jax
import jax.numpy as jnp
from jax import lax
from jax.experimental import pallas as pl
from jax.experimental.pallas import tpu as pltpu

F32 = jnp.float32
BF16 = jnp.bfloat16

D_MODEL = 1024
GRID_W = 64
N_Q_HEADS = 8
N_KV_HEADS = 2
HEAD_DIM = 64
Q_REP = N_Q_HEADS // N_KV_HEADS
ROPE_THETA = 10000.0
SSM_HEADS = 8
SSM_HEAD_DIM = 64
SSM_INNER = SSM_HEADS * SSM_HEAD_DIM
SSM_GROUPS = 2
HEADS_PER_GROUP = SSM_HEADS // SSM_GROUPS
D_STATE = 128
D_CONV = 5
CHUNK = 128
ATTN_DIM = N_Q_HEADS * HEAD_DIM
KV_DIM = N_KV_HEADS * HEAD_DIM
BC_DIM = SSM_GROUPS * D_STATE
CONV_DIM = SSM_INNER + 2 * BC_DIM
D_FF = ((8 * D_MODEL + 3 * 256 - 1) // (3 * 256)) * 256
N_MOD = 6
EPS = 1e-6

LANES = 128
HALO = 16
GROUP_W = HEADS_PER_GROUP * SSM_HEAD_DIM
VMEM_LIMIT = 48 * 1024 * 1024


def _dot(a, b):
    return jnp.dot(a, b, preferred_element_type=F32)


def _split3(x):
    hi = x.astype(BF16)
    r = x - hi.astype(F32)
    mid = r.astype(BF16)
    lo = (r - mid.astype(F32)).astype(BF16)
    return hi, mid, lo


def _dot_exact_rhs(sel, x):
    hi, mid, lo = _split3(x)
    return _dot(sel, hi) + _dot(sel, mid) + _dot(sel, lo)


def _dot_exact_lhs(x, sel):
    hi, mid, lo = _split3(x)
    return _dot(hi, sel) + _dot(mid, sel) + _dot(lo, sel)


def _silu(x):
    return x * jax.nn.sigmoid(x)


def _mod_kernel(c_ref, w_ref, b_ref, o_ref):
    c = c_ref[...]
    o_ref[...] = _dot(_silu(c), w_ref[...]) + b_ref[...]


def _mod(c_all, w_mod, b_mod):
    nb = c_all.shape[0]
    tn = 512
    return pl.pallas_call(
        _mod_kernel,
        grid=(N_MOD * D_MODEL // tn,),
        in_specs=[
            pl.BlockSpec((nb, D_MODEL), lambda j: (0, 0)),
            pl.BlockSpec((D_MODEL, tn), lambda j: (0, j)),
            pl.BlockSpec((1, tn), lambda j: (0, j)),
        ],
        out_specs=pl.BlockSpec((nb, tn), lambda j: (0, j)),
        out_shape=jax.ShapeDtypeStruct((nb, N_MOD * D_MODEL), F32),
        compiler_params=pltpu.CompilerParams(dimension_semantics=("parallel",)),
        name="mod",
    )(c_all, w_mod, b_mod.reshape(1, -1))


def _inproj_kernel(x_ref, mod_ref, g_ref, wq_ref, wk_ref, wv_ref, wz_ref, wx_ref, wdt_ref,
                   qg_ref, kg_ref, sq_ref, sk_ref, cos_ref, sina_ref, sinb_ref,
                   q_ref, kt_ref, v_ref, z_ref, xbc_ref, dt_ref):
    x = x_ref[0]
    ms = jnp.mean(x * x, axis=-1, keepdims=True)
    shift1 = mod_ref[0, 0:1, :]
    scale1 = mod_ref[0, 1:2, :]
    h = (x * lax.rsqrt(ms + EPS)) * g_ref[...] * (1.0 + scale1) + shift1
    hb = h.astype(BF16)

    cos = cos_ref[...]
    sina = sina_ref[...]
    sinb = sinb_ref[...]

    def head_norm(t, s_ref, gain):
        t2 = t * t
        hi = t2.astype(BF16)
        lo = (t2 - hi.astype(F32)).astype(BF16)
        msq = _dot(hi, s_ref[...]) + _dot(lo, s_ref[...])
        return t * lax.rsqrt(msq + EPS) * gain

    def rope(t):
        return t * cos + pltpu.roll(t, LANES - 16, 1) * sina + pltpu.roll(t, 16, 1) * sinb

    q = head_norm(_dot(hb, wq_ref[...]), sq_ref, qg_ref[...])
    scale = math.log2(math.e) / math.sqrt(HEAD_DIM)
    for j in range(ATTN_DIM // LANES):
        qj = rope(q[:, j * LANES:(j + 1) * LANES]) * scale
        q_ref[0, :, j * LANES:(j + 1) * LANES] = qj.astype(BF16)

    k = rope(head_norm(_dot(hb, wk_ref[...]), sk_ref, kg_ref[...]))
    kt_ref[0] = k.T.astype(BF16)

    v = _dot(hb, wv_ref[...])
    lane = lax.broadcasted_iota(jnp.int32, v.shape, 1)
    ones_col = jnp.where(lane == HEAD_DIM, 1.0, 0.0)
    v_ref[0, 0] = jnp.where(lane < HEAD_DIM, v, ones_col).astype(BF16)
    v_ref[0, 1] = jnp.where(lane < HEAD_DIM, pltpu.roll(v, HEAD_DIM, 1), ones_col).astype(BF16)

    z_ref[0] = _dot(hb, wz_ref[...]).astype(BF16)
    xbc_ref[0] = _dot(hb, wx_ref[...]).astype(BF16)
    dt_ref[0] = _dot(hb, wdt_ref[...])


def _inproj(x, mod3, boff, consts, tm=512):
    b, s, _ = x.shape
    (g, wq, wk, wv, wz, wx, wdt, qg, kg, sq, sk, cos, sina, sinb) = consts
    full = lambda a: pl.BlockSpec(a.shape, lambda bi, si: (0,) * a.ndim)
    tab = pl.BlockSpec((tm, LANES), lambda bi, si: (si, 0))
    return pl.pallas_call(
        _inproj_kernel,
        grid=(b, s // tm),
        in_specs=[
            pl.BlockSpec((1, tm, D_MODEL), lambda bi, si: (bi, si, 0)),
            pl.BlockSpec((1, N_MOD, D_MODEL), lambda bi, si: (bi + boff, 0, 0)),
            full(g), full(wq), full(wk), full(wv), full(wz), full(wx), full(wdt),
            full(qg), full(kg), full(sq), full(sk), tab, tab, tab,
        ],
        out_specs=[
            pl.BlockSpec((1, tm, ATTN_DIM), lambda bi, si: (bi, si, 0)),
            pl.BlockSpec((1, KV_DIM, tm), lambda bi, si: (bi, 0, si)),
            pl.BlockSpec((1, N_KV_HEADS, tm, LANES), lambda bi, si: (bi, 0, si, 0)),
            pl.BlockSpec((1, tm, SSM_INNER), lambda bi, si: (bi, si, 0)),
            pl.BlockSpec((1, tm, CONV_DIM), lambda bi, si: (bi, si, 0)),
            pl.BlockSpec((1, tm, LANES), lambda bi, si: (bi, si, 0)),
        ],
        out_shape=[
            jax.ShapeDtypeStruct((b, s, ATTN_DIM), BF16),
            jax.ShapeDtypeStruct((b, KV_DIM, s), BF16),
            jax.ShapeDtypeStruct((b, N_KV_HEADS, s, LANES), BF16),
            jax.ShapeDtypeStruct((b, s, SSM_INNER), BF16),
            jax.ShapeDtypeStruct((b, s, CONV_DIM), BF16),
            jax.ShapeDtypeStruct((b, s, LANES), F32),
        ],
        compiler_params=pltpu.CompilerParams(
            dimension_semantics=("parallel", "parallel"), vmem_limit_bytes=VMEM_LIMIT),
        name="inproj",
    )(x, mod3, g, wq, wk, wv, wz, wx, wdt, qg, kg, sq, sk, cos, sina, sinb)


def _attn_kernel(q_ref, kt_ref, v_ref, o_ref, q_scr, m_scr, acc_scr, *, tq, tk, nk):
    for r in range(Q_REP):
        q_scr[r * tq:(r + 1) * tq, :] = q_ref[0, :, r * HEAD_DIM:(r + 1) * HEAD_DIM]
    m_scr[...] = jnp.full(m_scr.shape, -jnp.inf, F32)
    acc_scr[...] = jnp.zeros(acc_scr.shape, F32)

    def body(k, carry):
        off = pl.multiple_of(k * tk, tk)
        s = _dot(q_scr[...], kt_ref[0, :, pl.ds(off, tk)])
        m_prev = m_scr[...]
        m_new = jnp.maximum(m_prev, jnp.max(s, axis=-1, keepdims=True))
        alpha = jnp.exp2(m_prev - m_new)
        p = jnp.concatenate(
            [jnp.exp2(s[:, j * LANES:(j + 1) * LANES] - m_new) for j in range(tk // LANES)], axis=1)
        acc_scr[...] = acc_scr[...] * alpha + _dot(p.astype(BF16), v_ref[0, 0, pl.ds(off, tk), :])
        m_scr[...] = m_new
        return carry

    lax.fori_loop(0, nk, body, 0)

    acc = acc_scr[...]
    o = acc[:, :HEAD_DIM] / acc[:, HEAD_DIM:HEAD_DIM + 1]
    for r in range(Q_REP):
        o_ref[0, :, r * HEAD_DIM:(r + 1) * HEAD_DIM] = o[r * tq:(r + 1) * tq].astype(BF16)


def _attn(q, kt, vext, tq=512, tk=1024):
    b, s, _ = q.shape
    gw = Q_REP * HEAD_DIM
    return pl.pallas_call(
        functools.partial(_attn_kernel, tq=tq, tk=tk, nk=s // tk),
        grid=(b, N_KV_HEADS, s // tq),
        in_specs=[
            pl.BlockSpec((1, tq, gw), lambda bi, gi, qi: (bi, qi, gi)),
            pl.BlockSpec((1, HEAD_DIM, s), lambda bi, gi, qi: (bi, gi, 0)),
            pl.BlockSpec((1, 1, s, LANES), lambda bi, gi, qi: (bi, gi, 0, 0)),
        ],
        out_specs=pl.BlockSpec((1, tq, gw), lambda bi, gi, qi: (bi, qi, gi)),
        out_shape=jax.ShapeDtypeStruct((b, s, ATTN_DIM), BF16),
        scratch_shapes=[
            pltpu.VMEM((Q_REP * tq, HEAD_DIM), BF16),
            pltpu.VMEM((Q_REP * tq, LANES), F32),
            pltpu.VMEM((Q_REP * tq, LANES), F32),
        ],
        compiler_params=pltpu.CompilerParams(
            dimension_semantics=("parallel", "parallel", "parallel"),
            vmem_limit_bytes=VMEM_LIMIT),
        name="attn",
    )(q, kt, vext)


def _softplus(x):
    return jnp.maximum(x, 0.0) + jnp.log1p(jnp.exp(-jnp.abs(x)))


def _ssd_chunk(act, dt_raw, dtb_row, a_row, ex2, h_scr, reverse):
    xs = act[:, :SSM_INNER]
    dtp = _softplus(dt_raw + dtb_row)
    a_dt = dtp * a_row

    li = lax.broadcasted_iota(jnp.int32, (CHUNK, CHUNK), 0)
    mi = lax.broadcasted_iota(jnp.int32, (CHUNK, CHUNK), 1)
    keep = (mi >= li) if reverse else (mi <= li)
    tri = jnp.where(keep, 1.0, 0.0).astype(BF16)
    cs = _dot_exact_rhs(tri, a_dt)
    cs_t = cs.T
    dtp_t = dtp.T
    end = 0 if reverse else CHUNK - 1

    e_out = jnp.exp(cs)
    qw = dtp * jnp.exp(cs[end:end + 1, :] - cs)
    both = jnp.concatenate([e_out, qw], axis=0)
    hi = both.astype(BF16)
    mid = (both - hi.astype(F32)).astype(BF16)
    expd = _dot(jnp.concatenate([hi, mid], axis=1), ex2)
    e_out_x = expd[:CHUNK]
    qw_x = expd[CHUNK:]
    dec_chunk = e_out_x[end:end + 1, :]
    xs_b = xs.astype(BF16)
    xw = (xs * qw_x).astype(BF16)

    base = SSM_HEADS if reverse else 0
    lane_head = lax.broadcasted_iota(jnp.int32, (CHUNK, GROUP_W), 1) // SSM_HEAD_DIM
    ys = []
    for g in range(SSM_GROUPS):
        gs = slice(g * GROUP_W, (g + 1) * GROUP_W)
        bm = act[:, SSM_INNER + g * D_STATE:SSM_INNER + (g + 1) * D_STATE]
        cm_b = act[:, SSM_INNER + BC_DIM + g * D_STATE:
                   SSM_INNER + BC_DIM + (g + 1) * D_STATE].astype(BF16)
        cb = lax.dot_general(cm_b, bm.astype(BF16), (((1,), (1,)), ((), ())),
                             preferred_element_type=F32)
        m_parts = []
        rhs_parts = []
        for r in range(HEADS_PER_GROUP):
            j = base + g * HEADS_PER_GROUP + r
            diff = cs[:, j:j + 1] - cs_t[j:j + 1, :]
            dec = jnp.exp(jnp.where(keep, diff, -jnp.inf))
            m_parts.append((cb * dec * dtp_t[j:j + 1, :]).astype(BF16))
            rhs_parts.append(jnp.where(lane_head == r, xs_b[:, gs], jnp.zeros((), BF16)))
        y_diag = _dot(jnp.concatenate(m_parts, axis=1), jnp.concatenate(rhs_parts, axis=0))
        h_old = h_scr[g]
        y_off = _dot(cm_b, h_old.astype(BF16)) * e_out_x[:, gs]
        s_new = _dot(bm.T.astype(BF16), xw[:, gs])
        h_scr[g] = h_old * dec_chunk[:, gs] + s_new
        ys.append(y_diag + y_off)
    return jnp.concatenate(ys, axis=1)


def _decay_rates(alog_ref):
    lane = lax.broadcasted_iota(jnp.int32, (1, LANES), 1)
    return jnp.where(lane < 2 * SSM_HEADS, -jnp.exp(alog_ref[...]), 0.0)


def _ssd_fwd_kernel(xbc_ref, prev_ref, next_ref, dt_ref, cw_ref, cb_ref, dtb_ref, alog_ref,
                    ex2_ref, yf_ref, act_ref, h_scr, xe_scr, *, nsteps, nchunk):
    step = pl.program_id(1)

    @pl.when(step == 0)
    def _():
        h_scr[...] = jnp.zeros(h_scr.shape, F32)

    rows = nchunk * CHUNK
    prev = prev_ref[0, HALO - 8:HALO, :].astype(F32)
    nxt = next_ref[0, 0:8, :].astype(F32)
    xe_scr[0:8, :] = jnp.where(step > 0, prev, 0.0)
    xe_scr[8:8 + rows, :] = xbc_ref[0].astype(F32)
    xe_scr[8 + rows:16 + rows, :] = jnp.where(step < nsteps - 1, nxt, 0.0)
    pad = D_CONV // 2
    a_row = _decay_rates(alog_ref)
    for c in range(nchunk):
        r0 = 8 + c * CHUNK - pad
        acc = jnp.broadcast_to(cb_ref[...], (CHUNK, CONV_DIM))
        for k in range(D_CONV):
            acc = acc + xe_scr[r0 + k:r0 + k + CHUNK, :] * cw_ref[k:k + 1, :]
        act = _silu(acc)
        sl = slice(c * CHUNK, (c + 1) * CHUNK)
        act_ref[0, sl, :] = act.astype(BF16)
        y = _ssd_chunk(act, dt_ref[0, sl, :], dtb_ref[...], a_row, ex2_ref[...], h_scr, False)
        yf_ref[0, sl, :] = y.astype(BF16)


def _ssd_bwd_kernel(act_ref, dt_ref, dtb_ref, alog_ref, ex2_ref, yf_ref, z_ref, dskip_ref, ng_ref,
                    o_ref, h_scr, *, nchunk):
    @pl.when(pl.program_id(1) == 0)
    def _():
        h_scr[...] = jnp.zeros(h_scr.shape, F32)

    a_row = _decay_rates(alog_ref)
    for c in reversed(range(nchunk)):
        sl = slice(c * CHUNK, (c + 1) * CHUNK)
        act = act_ref[0, sl, :].astype(F32)
        y = _ssd_chunk(act, dt_ref[0, sl, :], dtb_ref[...], a_row, ex2_ref[...], h_scr, True)
        y = yf_ref[0, sl, :].astype(F32) + y + dskip_ref[...] * act[:, :SSM_INNER]
        gated = y * _silu(z_ref[0, sl, :].astype(F32))
        ms = jnp.mean(gated * gated, axis=-1, keepdims=True)
        o_ref[0, sl, :] = ((gated * lax.rsqrt(ms + EPS)) * ng_ref[...]).astype(BF16)


def _ssd(xbc, dt, z, consts, nchunk=4):
    b, s, _ = xbc.shape
    rows = nchunk * CHUNK
    nsteps = s // rows
    per = rows // HALO
    last_halo = s // HALO - 1
    cw, cbias, dtb, alog, ex2_f, ex2_b, dskip, ng = consts
    full = lambda a: pl.BlockSpec(a.shape, lambda bi, c: (0,) * a.ndim)
    tok = lambda w: pl.BlockSpec((1, rows, w), lambda bi, c: (bi, c, 0))
    rtok = lambda w: pl.BlockSpec((1, rows, w), lambda bi, c: (bi, nsteps - 1 - c, 0))
    state = pltpu.VMEM((SSM_GROUPS, D_STATE, GROUP_W), F32)
    params = pltpu.CompilerParams(
        dimension_semantics=("parallel", "arbitrary"), vmem_limit_bytes=VMEM_LIMIT)
    yf, act = pl.pallas_call(
        functools.partial(_ssd_fwd_kernel, nsteps=nsteps, nchunk=nchunk),
        grid=(b, nsteps),
        in_specs=[
            tok(CONV_DIM),
            pl.BlockSpec((1, HALO, CONV_DIM), lambda bi, c: (bi, jnp.maximum(c * per - 1, 0), 0)),
            pl.BlockSpec((1, HALO, CONV_DIM),
                         lambda bi, c: (bi, jnp.minimum((c + 1) * per, last_halo), 0)),
            tok(LANES), full(cw), full(cbias), full(dtb), full(alog), full(ex2_f),
        ],
        out_specs=[tok(SSM_INNER), tok(CONV_DIM)],
        out_shape=[jax.ShapeDtypeStruct((b, s, SSM_INNER), BF16),
                   jax.ShapeDtypeStruct((b, s, CONV_DIM), BF16)],
        scratch_shapes=[state, pltpu.VMEM((rows + 16, CONV_DIM), F32)],
        compiler_params=params,
        name="ssd_fwd",
    )(xbc, xbc, xbc, dt, cw, cbias, dtb, alog, ex2_f)
    return pl.pallas_call(
        functools.partial(_ssd_bwd_kernel, nchunk=nchunk),
        grid=(b, nsteps),
        in_specs=[rtok(CONV_DIM), rtok(LANES), full(dtb), full(alog), full(ex2_b),
                  rtok(SSM_INNER), rtok(SSM_INNER), full(dskip), full(ng)],
        out_specs=rtok(SSM_INNER),
        out_shape=jax.ShapeDtypeStruct((b, s, SSM_INNER), BF16),
        scratch_shapes=[state],
        compiler_params=params,
        name="ssd_bwd",
    )(act, dt, dtb, alog, ex2_b, yf, z, dskip, ng)


def _out_ffn_kernel(x_ref, attn_ref, ssd_ref, mod_ref, wo_a_ref, wo_s_ref, g_ref, wg_ref, wu_ref,
                    wd_ref, o_ref):
    x = x_ref[0]
    gate1 = mod_ref[0, 2:3, :]
    shift2 = mod_ref[0, 3:4, :]
    scale2 = mod_ref[0, 4:5, :]
    gate2 = mod_ref[0, 5:6, :]
    mix = _dot(attn_ref[0], wo_a_ref[...]) + _dot(ssd_ref[0], wo_s_ref[...])
    x1 = x + gate1 * mix
    ms = jnp.mean(x1 * x1, axis=-1, keepdims=True)
    h2 = ((x1 * lax.rsqrt(ms + EPS)) * g_ref[...] * (1.0 + scale2) + shift2).astype(BF16)
    act = (_silu(_dot(h2, wg_ref[...])) * _dot(h2, wu_ref[...])).astype(BF16)
    o_ref[0] = x1 + gate2 * _dot(act, wd_ref[...])


def _out_ffn(x, attn, ssd, mod3, boff, consts, tm=512):
    b, s, _ = x.shape
    wo_a, wo_s, g, wg, wu, wd = consts
    full = lambda a: pl.BlockSpec(a.shape, lambda bi, si: (0,) * a.ndim,
                                  pipeline_mode=pl.Buffered(1))
    tok = lambda w: pl.BlockSpec((1, tm, w), lambda bi, si: (bi, si, 0))
    return pl.pallas_call(
        _out_ffn_kernel,
        grid=(b, s // tm),
        in_specs=[
            tok(D_MODEL), tok(ATTN_DIM), tok(SSM_INNER),
            pl.BlockSpec((1, N_MOD, D_MODEL), lambda bi, si: (bi + boff, 0, 0)),
            full(wo_a), full(wo_s), full(g), full(wg), full(wu), full(wd),
        ],
        out_specs=tok(D_MODEL),
        out_shape=jax.ShapeDtypeStruct((b, s, D_MODEL), F32),
        compiler_params=pltpu.CompilerParams(
            dimension_semantics=("parallel", "parallel"), vmem_limit_bytes=VMEM_LIMIT),
        name="out_ffn",
    )(x, attn, ssd, mod3, wo_a, wo_s, g, wg, wu, wd)


def _rope_tables(s):
    t = jnp.arange(s)
    half = HEAD_DIM // 4
    freqs = ROPE_THETA ** (-jnp.arange(half, dtype=F32) / half)
    ang_r = (t // GRID_W).astype(F32)[:, None] * freqs[None, :]
    ang_c = (t % GRID_W).astype(F32)[:, None] * freqs[None, :]
    ang = jnp.concatenate([ang_r, ang_r, ang_c, ang_c], axis=-1)
    first_half = (jnp.arange(HEAD_DIM) % (2 * half)) < half
    cos = jnp.cos(ang)
    sin = jnp.sin(ang)
    sina = jnp.where(first_half, -sin, 0.0)
    sinb = jnp.where(first_half, 0.0, sin)
    rep = LANES // HEAD_DIM
    return tuple(jnp.tile(a, (1, rep)) for a in (cos, sina, sinb))


def _block_mean(width):
    idx = jnp.arange(width) // HEAD_DIM
    return jnp.where(idx[:, None] == idx[None, :], 1.0 / HEAD_DIM, 0.0).astype(BF16)


def _pad_lanes(row):
    return jnp.pad(row.astype(F32), (0, LANES - row.shape[0])).reshape(1, LANES)


def kernel(x_prompt, x_sample, c_prompt, c_sample, w_mod, b_mod, norm_mix_g, w_in, q_norm_g, k_norm_g, conv_w, conv_b, dt_bias_fwd, a_log_fwd, dt_bias_bwd, a_log_bwd, d_skip, ssd_norm_g, w_out, norm_ffn_g, w_gate_up, w_down):
    assert w_mod.shape[0] == 1
    l = 0
    c_all = jnp.concatenate([c_prompt, c_sample], axis=0)
    mod3 = _mod(c_all, w_mod[l], b_mod[l]).reshape(c_all.shape[0], N_MOD, D_MODEL)

    w = w_in[l].astype(BF16)
    o0 = 0
    o1 = o0 + ATTN_DIM
    o2 = o1 + KV_DIM
    o3 = o2 + KV_DIM
    o4 = o3 + SSM_INNER
    o5 = o4 + CONV_DIM
    wdt = jnp.pad(w[:, o5:], ((0, 0), (0, LANES - 2 * SSM_HEADS)))
    in_consts = (
        norm_mix_g[l].reshape(1, -1), w[:, o0:o1], w[:, o1:o2], w[:, o2:o3], w[:, o3:o4],
        w[:, o4:o5], wdt,
        jnp.tile(q_norm_g[l], N_Q_HEADS).reshape(1, -1),
        jnp.tile(k_norm_g[l], N_KV_HEADS).reshape(1, -1),
        _block_mean(ATTN_DIM), _block_mean(KV_DIM),
    )
    head_of_lane = jnp.arange(SSM_INNER) // SSM_HEAD_DIM
    expand_f = (jnp.arange(LANES)[:, None] == head_of_lane[None, :]).astype(BF16)
    expand_b = (jnp.arange(LANES)[:, None] == head_of_lane[None, :] + SSM_HEADS).astype(BF16)
    ssd_consts = (
        conv_w[l], conv_b[l].reshape(1, -1),
        _pad_lanes(jnp.concatenate([dt_bias_fwd[l], dt_bias_bwd[l]])),
        _pad_lanes(jnp.concatenate([a_log_fwd[l], a_log_bwd[l]])),
        jnp.concatenate([expand_f, expand_f], axis=0), jnp.concatenate([expand_b, expand_b], axis=0),
        jnp.repeat(d_skip[l], SSM_HEAD_DIM).reshape(1, -1), ssd_norm_g[l].reshape(1, -1),
    )
    wo = w_out[l].astype(BF16)
    wgu = w_gate_up[l].astype(BF16)
    ffn_consts = (wo[:ATTN_DIM], wo[ATTN_DIM:], norm_ffn_g[l].reshape(1, -1),
                  wgu[:, :D_FF], wgu[:, D_FF:], w_down[l].astype(BF16))

    def run(x, boff):
        s = x.shape[1]
        q, kt, vext, z, xbc, dt = _inproj(x, mod3, boff, in_consts + _rope_tables(s))
        attn = _attn(q, kt, vext)
        ssd = _ssd(xbc, dt, z, ssd_consts)
        return _out_ffn(x, attn, ssd, mod3, boff, ffn_consts)

    return run(x_prompt, 0), run(x_sample, c_prompt.shape[0])
```

```python
import functools
import math

import jax
import jax.numpy as jnp
from jax import lax
from jax.experimental import pallas as pl
from jax.experimental.pallas import tpu as pltpu

F32 = jnp.float32
BF16 = jnp.bfloat16

D_MODEL = 1024
GRID_W = 64
N_Q_HEADS = 8
N_KV_HEADS = 2
HEAD_DIM = 64
Q_REP = N_Q_HEADS // N_KV_HEADS
ROPE_THETA = 10000.0
SSM_HEADS = 8
SSM_HEAD_DIM = 64
SSM_INNER = SSM_HEADS * SSM_HEAD_DIM
SSM_GROUPS = 2
HEADS_PER_GROUP = SSM_HEADS // SSM_GROUPS
D_STATE = 128
D_CONV = 5
CHUNK = 128
ATTN_DIM = N_Q_HEADS * HEAD_DIM
KV_DIM = N_KV_HEADS * HEAD_DIM
BC_DIM = SSM_GROUPS * D_STATE
CONV_DIM = SSM_INNER + 2 * BC_DIM
D_FF = ((8 * D_MODEL + 3 * 256 - 1) // (3 * 256)) * 256
N_MOD = 6
EPS = 1e-6

LANES = 128
HALO = 16
GROUP_W = HEADS_PER_GROUP * SSM_HEAD_DIM
VMEM_LIMIT = 48 * 1024 * 1024


def _dot(a, b):
    return jnp.dot(a, b, preferred_element_type=F32)


def _split3(x):
    hi = x.astype(BF16)
    r = x - hi.astype(F32)
    mid = r.astype(BF16)
    lo = (r - mid.astype(F32)).astype(BF16)
    return hi, mid, lo


def _dot_exact_rhs(sel, x):
    hi, mid, lo = _split3(x)
    return _dot(sel, hi) + _dot(sel, mid) + _dot(sel, lo)


def _dot_exact_lhs(x, sel):
    hi, mid, lo = _split3(x)
    return _dot(hi, sel) + _dot(mid, sel) + _dot(lo, sel)


def _silu(x):
    h = 0.5 * x
    return h + h * jnp.tanh(h)


def _mod_kernel(c_ref, w_ref, b_ref, o_ref):
    c = c_ref[...]
    o_ref[...] = _dot(_silu(c), w_ref[...]) + b_ref[...]


def _mod(c_all, w_mod, b_mod):
    nb = c_all.shape[0]
    tn = 512
    return pl.pallas_call(
        _mod_kernel,
        grid=(N_MOD * D_MODEL // tn,),
        in_specs=[
            pl.BlockSpec((nb, D_MODEL), lambda j: (0, 0)),
            pl.BlockSpec((D_MODEL, tn), lambda j: (0, j)),
            pl.BlockSpec((1, tn), lambda j: (0, j)),
        ],
        out_specs=pl.BlockSpec((nb, tn), lambda j: (0, j)),
        out_shape=jax.ShapeDtypeStruct((nb, N_MOD * D_MODEL), F32),
        compiler_params=pltpu.CompilerParams(dimension_semantics=("parallel",)),
        name="mod",
    )(c_all, w_mod, b_mod.reshape(1, -1))


def _inproj_kernel(x_ref, mod_ref, g_ref, wq_ref, wk_ref, wv_ref, wz_ref, wx_ref, wdt_ref,
                   qg_ref, kg_ref, sq_ref, sk_ref, cos_ref, sina_ref, sinb_ref,
                   q_ref, kt_ref, v_ref, z_ref, xbc_ref, dt_ref):
    x = x_ref[0]
    ms = jnp.mean(x * x, axis=-1, keepdims=True)
    shift1 = mod_ref[0, 0:1, :]
    scale1 = mod_ref[0, 1:2, :]
    h = (x * lax.rsqrt(ms + EPS)) * g_ref[...] * (1.0 + scale1) + shift1
    hb = h.astype(BF16)

    cos = cos_ref[...]
    sina = sina_ref[...]
    sinb = sinb_ref[...]

    def head_norm(t, s_ref, gain):
        t2 = t * t
        hi = t2.astype(BF16)
        lo = (t2 - hi.astype(F32)).astype(BF16)
        msq = _dot(hi, s_ref[...]) + _dot(lo, s_ref[...])
        return t * lax.rsqrt(msq + EPS) * gain

    def rope(t):
        return t * cos + pltpu.roll(t, LANES - 16, 1) * sina + pltpu.roll(t, 16, 1) * sinb

    q = head_norm(_dot(hb, wq_ref[...]), sq_ref, qg_ref[...])
    scale = math.log2(math.e) / math.sqrt(HEAD_DIM)
    for j in range(ATTN_DIM // LANES):
        qj = rope(q[:, j * LANES:(j + 1) * LANES]) * scale
        q_ref[0, :, j * LANES:(j + 1) * LANES] = qj.astype(BF16)

    k = rope(head_norm(_dot(hb, wk_ref[...]), sk_ref, kg_ref[...]))
    kt_ref[0] = k.T.astype(BF16)

    v = _dot(hb, wv_ref[...])
    lane = lax.broadcasted_iota(jnp.int32, v.shape, 1)
    ones_col = jnp.where(lane == HEAD_DIM, 1.0, 0.0)
    v_ref[0, 0] = jnp.where(lane < HEAD_DIM, v, ones_col).astype(BF16)
    v_ref[0, 1] = jnp.where(lane < HEAD_DIM, pltpu.roll(v, HEAD_DIM, 1), ones_col).astype(BF16)

    z_ref[0] = _dot(hb, wz_ref[...]).astype(BF16)
    xbc_ref[0] = _dot(hb, wx_ref[...]).astype(BF16)
    dt_ref[0] = _dot(hb, wdt_ref[...])


def _inproj(x, mod3, boff, consts, tm=512):
    b, s, _ = x.shape
    (g, wq, wk, wv, wz, wx, wdt, qg, kg, sq, sk, cos, sina, sinb) = consts
    full = lambda a: pl.BlockSpec(a.shape, lambda bi, si: (0,) * a.ndim)
    tab = pl.BlockSpec((tm, LANES), lambda bi, si: (si, 0))
    return pl.pallas_call(
        _inproj_kernel,
        grid=(b, s // tm),
        in_specs=[
            pl.BlockSpec((1, tm, D_MODEL), lambda bi, si: (bi, si, 0)),
            pl.BlockSpec((1, N_MOD, D_MODEL), lambda bi, si: (bi + boff, 0, 0)),
            full(g), full(wq), full(wk), full(wv), full(wz), full(wx), full(wdt),
            full(qg), full(kg), full(sq), full(sk), tab, tab, tab,
        ],
        out_specs=[
            pl.BlockSpec((1, tm, ATTN_DIM), lambda bi, si: (bi, si, 0)),
            pl.BlockSpec((1, KV_DIM, tm), lambda bi, si: (bi, 0, si)),
            pl.BlockSpec((1, N_KV_HEADS, tm, LANES), lambda bi, si: (bi, 0, si, 0)),
            pl.BlockSpec((1, tm, SSM_INNER), lambda bi, si: (bi, si, 0)),
            pl.BlockSpec((1, tm, CONV_DIM), lambda bi, si: (bi, si, 0)),
            pl.BlockSpec((1, tm, LANES), lambda bi, si: (bi, si, 0)),
        ],
        out_shape=[
            jax.ShapeDtypeStruct((b, s, ATTN_DIM), BF16),
            jax.ShapeDtypeStruct((b, KV_DIM, s), BF16),
            jax.ShapeDtypeStruct((b, N_KV_HEADS, s, LANES), BF16),
            jax.ShapeDtypeStruct((b, s, SSM_INNER), BF16),
            jax.ShapeDtypeStruct((b, s, CONV_DIM), BF16),
            jax.ShapeDtypeStruct((b, s, LANES), F32),
        ],
        compiler_params=pltpu.CompilerParams(
            dimension_semantics=("parallel", "parallel"), vmem_limit_bytes=VMEM_LIMIT),
        name="inproj",
    )(x, mod3, g, wq, wk, wv, wz, wx, wdt, qg, kg, sq, sk, cos, sina, sinb)


def _attn_kernel(q_ref, kt_ref, v_ref, o_ref, q_scr, m_scr, acc_scr, *, tq, tk, nk):
    for r in range(Q_REP):
        q_scr[r * tq:(r + 1) * tq, :] = q_ref[0, :, r * HEAD_DIM:(r + 1) * HEAD_DIM]
    m_scr[...] = jnp.full(m_scr.shape, -jnp.inf, F32)
    acc_scr[...] = jnp.zeros(acc_scr.shape, F32)

    def body(k, carry):
        off = pl.multiple_of(k * tk, tk)
        s = _dot(q_scr[...], kt_ref[0, :, pl.ds(off, tk)])
        m_prev = m_scr[...]
        m_new = jnp.maximum(m_prev, jnp.max(s, axis=-1, keepdims=True))
        alpha = jnp.exp2(m_prev - m_new)
        p = jnp.concatenate(
            [jnp.exp2(s[:, j * LANES:(j + 1) * LANES] - m_new) for j in range(tk // LANES)], axis=1)
        acc_scr[...] = acc_scr[...] * alpha + _dot(p.astype(BF16), v_ref[0, 0, pl.ds(off, tk), :])
        m_scr[...] = m_new
        return carry

    lax.fori_loop(0, nk, body, 0, unroll=4)

    acc = acc_scr[...]
    o = acc[:, :HEAD_DIM] / acc[:, HEAD_DIM:HEAD_DIM + 1]
    for r in range(Q_REP):
        o_ref[0, :, r * HEAD_DIM:(r + 1) * HEAD_DIM] = o[r * tq:(r + 1) * tq].astype(BF16)


def _attn(q, kt, vext, tq=512, tk=1024):
    b, s, _ = q.shape
    gw = Q_REP * HEAD_DIM
    return pl.pallas_call(
        functools.partial(_attn_kernel, tq=tq, tk=tk, nk=s // tk),
        grid=(b, N_KV_HEADS, s // tq),
        in_specs=[
            pl.BlockSpec((1, tq, gw), lambda bi, gi, qi: (bi, qi, gi)),
            pl.BlockSpec((1, HEAD_DIM, s), lambda bi, gi, qi: (bi, gi, 0)),
            pl.BlockSpec((1, 1, s, LANES), lambda bi, gi, qi: (bi, gi, 0, 0)),
        ],
        out_specs=pl.BlockSpec((1, tq, gw), lambda bi, gi, qi: (bi, qi, gi)),
        out_shape=jax.ShapeDtypeStruct((b, s, ATTN_DIM), BF16),
        scratch_shapes=[
            pltpu.VMEM((Q_REP * tq, HEAD_DIM), BF16),
            pltpu.VMEM((Q_REP * tq, LANES), F32),
            pltpu.VMEM((Q_REP * tq, LANES), F32),
        ],
        compiler_params=pltpu.CompilerParams(
            dimension_semantics=("parallel", "parallel", "parallel"),
            vmem_limit_bytes=VMEM_LIMIT),
        name="attn",
    )(q, kt, vext)


def _softplus(x):
    return jnp.maximum(x, 0.0) + jnp.log(1.0 + jnp.exp(-jnp.abs(x)))


def _ssd_chunk(act, dt_raw, dtb_row, a_row, ex2, h_scr, reverse):
    xs = act[:, :SSM_INNER]
    dtp = _softplus(dt_raw + dtb_row)
    a_dt = dtp * a_row

    li = lax.broadcasted_iota(jnp.int32, (CHUNK, CHUNK), 0)
    mi = lax.broadcasted_iota(jnp.int32, (CHUNK, CHUNK), 1)
    keep = (mi >= li) if reverse else (mi <= li)
    tri = jnp.where(keep, 1.0, 0.0).astype(BF16)
    cs = _dot_exact_rhs(tri, a_dt)
    cs_t = cs.T
    dtp_t = dtp.T
    end = 0 if reverse else CHUNK - 1

    e_out = jnp.exp(cs)
    qw = dtp * jnp.exp(cs[end:end + 1, :] - cs)
    both = jnp.concatenate([e_out, qw], axis=0)
    hi = both.astype(BF16)
    mid = (both - hi.astype(F32)).astype(BF16)
    expd = _dot(jnp.concatenate([hi, mid], axis=1), ex2)
    e_out_x = expd[:CHUNK]
    qw_x = expd[CHUNK:]
    dec_chunk = e_out_x[end:end + 1, :]
    xs_b = xs.astype(BF16)
    xw = (xs * qw_x).astype(BF16)

    base = SSM_HEADS if reverse else 0
    lane_head = lax.broadcasted_iota(jnp.int32, (CHUNK, GROUP_W), 1) // SSM_HEAD_DIM
    ys = []
    for g in range(SSM_GROUPS):
        gs = slice(g * GROUP_W, (g + 1) * GROUP_W)
        bm = act[:, SSM_INNER + g * D_STATE:SSM_INNER + (g + 1) * D_STATE]
        cm_b = act[:, SSM_INNER + BC_DIM + g * D_STATE:
                   SSM_INNER + BC_DIM + (g + 1) * D_STATE].astype(BF16)
        cb = lax.dot_general(cm_b, bm.astype(BF16), (((1,), (1,)), ((), ())),
                             preferred_element_type=F32)
        m_parts = []
        rhs_parts = []
        for r in range(HEADS_PER_GROUP):
            j = base + g * HEADS_PER_GROUP + r
            diff = cs[:, j:j + 1] - cs_t[j:j + 1, :]
            dec = jnp.exp(jnp.where(keep, diff, -jnp.inf))
            m_parts.append((cb * dec * dtp_t[j:j + 1, :]).astype(BF16))
            rhs_parts.append(jnp.where(lane_head == r, xs_b[:, gs], jnp.zeros((), BF16)))
        y_diag = _dot(jnp.concatenate(m_parts, axis=1), jnp.concatenate(rhs_parts, axis=0))
        h_old = h_scr[g]
        y_off = _dot(cm_b, h_old.astype(BF16)) * e_out_x[:, gs]
        s_new = _dot(bm.T.astype(BF16), xw[:, gs])
        h_scr[g] = h_old * dec_chunk[:, gs] + s_new
        ys.append(y_diag + y_off)
    return jnp.concatenate(ys, axis=1)


def _decay_rates(alog_ref):
    lane = lax.broadcasted_iota(jnp.int32, (1, LANES), 1)
    return jnp.where(lane < 2 * SSM_HEADS, -jnp.exp(alog_ref[...]), 0.0)


def _ssd_fwd_kernel(xbc_ref, prev_ref, next_ref, dt_ref, shift_ref, cw_ref, cb_ref, dtb_ref,
                    alog_ref, ex2_ref, yf_ref, act_ref, h_scr, xe_scr, *, nsteps, nchunk):
    step = pl.program_id(1)

    @pl.when(step == 0)
    def _():
        h_scr[...] = jnp.zeros(h_scr.shape, F32)

    rows = nchunk * CHUNK
    zero = jnp.zeros((), BF16)
    xe_scr[0:HALO, :] = jnp.where(step > 0, prev_ref[0], zero)
    xe_scr[HALO:HALO + rows, :] = xbc_ref[0]
    xe_scr[HALO + rows:2 * HALO + rows, :] = jnp.where(step < nsteps - 1, next_ref[0], zero)
    a_row = _decay_rates(alog_ref)
    for c in range(nchunk):
        window = xe_scr[c * CHUNK:c * CHUNK + CHUNK + 2 * HALO, :]
        taps = _dot(shift_ref[...], window)
        acc = jnp.broadcast_to(cb_ref[...], (CHUNK, CONV_DIM))
        for k in range(D_CONV):
            acc = acc + taps[k * CHUNK:(k + 1) * CHUNK, :] * cw_ref[k:k + 1, :]
        act = _silu(acc)
        sl = slice(c * CHUNK, (c + 1) * CHUNK)
        act_ref[0, sl, :] = act.astype(BF16)
        y = _ssd_chunk(act, dt_ref[0, sl, :], dtb_ref[...], a_row, ex2_ref[...], h_scr, False)
        yf_ref[0, sl, :] = y.astype(BF16)


def _ssd_bwd_kernel(act_ref, dt_ref, dtb_ref, alog_ref, ex2_ref, yf_ref, z_ref, dskip_ref, ng_ref,
                    o_ref, h_scr, *, nchunk):
    @pl.when(pl.program_id(1) == 0)
    def _():
        h_scr[...] = jnp.zeros(h_scr.shape, F32)

    a_row = _decay_rates(alog_ref)
    for c in reversed(range(nchunk)):
        sl = slice(c * CHUNK, (c + 1) * CHUNK)
        act = act_ref[0, sl, :].astype(F32)
        y = _ssd_chunk(act, dt_ref[0, sl, :], dtb_ref[...], a_row, ex2_ref[...], h_scr, True)
        y = yf_ref[0, sl, :].astype(F32) + y + dskip_ref[...] * act[:, :SSM_INNER]
        gated = y * _silu(z_ref[0, sl, :].astype(F32))
        ms = jnp.mean(gated * gated, axis=-1, keepdims=True)
        o_ref[0, sl, :] = ((gated * lax.rsqrt(ms + EPS)) * ng_ref[...]).astype(BF16)


def _ssd(xbc, dt, z, consts, nchunk=4):
    b, s, _ = xbc.shape
    rows = nchunk * CHUNK
    nsteps = s // rows
    per = rows // HALO
    last_halo = s // HALO - 1
    shift, cw, cbias, dtb, alog, ex2_f, ex2_b, dskip, ng = consts
    full = lambda a: pl.BlockSpec(a.shape, lambda bi, c: (0,) * a.ndim)
    tok = lambda w: pl.BlockSpec((1, rows, w), lambda bi, c: (bi, c, 0))
    rtok = lambda w: pl.BlockSpec((1, rows, w), lambda bi, c: (bi, nsteps - 1 - c, 0))
    state = pltpu.VMEM((SSM_GROUPS, D_STATE, GROUP_W), F32)
    params = pltpu.CompilerParams(
        dimension_semantics=("parallel", "arbitrary"), vmem_limit_bytes=VMEM_LIMIT)
    yf, act = pl.pallas_call(
        functools.partial(_ssd_fwd_kernel, nsteps=nsteps, nchunk=nchunk),
        grid=(b, nsteps),
        in_specs=[
            tok(CONV_DIM),
            pl.BlockSpec((1, HALO, CONV_DIM), lambda bi, c: (bi, jnp.maximum(c * per - 1, 0), 0)),
            pl.BlockSpec((1, HALO, CONV_DIM),
                         lambda bi, c: (bi, jnp.minimum((c + 1) * per, last_halo), 0)),
            tok(LANES), full(shift), full(cw), full(cbias), full(dtb), full(alog), full(ex2_f),
        ],
        out_specs=[tok(SSM_INNER), tok(CONV_DIM)],
        out_shape=[jax.ShapeDtypeStruct((b, s, SSM_INNER), BF16),
                   jax.ShapeDtypeStruct((b, s, CONV_DIM), BF16)],
        scratch_shapes=[state, pltpu.VMEM((rows + 2 * HALO, CONV_DIM), BF16)],
        compiler_params=params,
        name="ssd_fwd",
    )(xbc, xbc, xbc, dt, shift, cw, cbias, dtb, alog, ex2_f)
    return pl.pallas_call(
        functools.partial(_ssd_bwd_kernel, nchunk=nchunk),
        grid=(b, nsteps),
        in_specs=[rtok(CONV_DIM), rtok(LANES), full(dtb), full(alog), full(ex2_b),
                  rtok(SSM_INNER), rtok(SSM_INNER), full(dskip), full(ng)],
        out_specs=rtok(SSM_INNER),
        out_shape=jax.ShapeDtypeStruct((b, s, SSM_INNER), BF16),
        scratch_shapes=[state],
        compiler_params=params,
        name="ssd_bwd",
    )(act, dt, dtb, alog, ex2_b, yf, z, dskip, ng)


def _out_ffn_kernel(x_ref, attn_ref, ssd_ref, mod_ref, wo_a_ref, wo_s_ref, g_ref, wg_ref, wu_ref,
                    wd_ref, o_ref):
    x = x_ref[0]
    gate1 = mod_ref[0, 2:3, :]
    shift2 = mod_ref[0, 3:4, :]
    scale2 = mod_ref[0, 4:5, :]
    gate2 = mod_ref[0, 5:6, :]
    mix = _dot(attn_ref[0], wo_a_ref[...]) + _dot(ssd_ref[0], wo_s_ref[...])
    x1 = x + gate1 * mix
    ms = jnp.mean(x1 * x1, axis=-1, keepdims=True)
    h2 = ((x1 * lax.rsqrt(ms + EPS)) * g_ref[...] * (1.0 + scale2) + shift2).astype(BF16)
    act = (_silu(_dot(h2, wg_ref[...])) * _dot(h2, wu_ref[...])).astype(BF16)
    o_ref[0] = x1 + gate2 * _dot(act, wd_ref[...])


def _out_ffn(x, attn, ssd, mod3, boff, consts, tm=512):
    b, s, _ = x.shape
    wo_a, wo_s, g, wg, wu, wd = consts
    full = lambda a: pl.BlockSpec(a.shape, lambda bi, si: (0,) * a.ndim,
                                  pipeline_mode=pl.Buffered(1))
    tok = lambda w: pl.BlockSpec((1, tm, w), lambda bi, si: (bi, si, 0))
    return pl.pallas_call(
        _out_ffn_kernel,
        grid=(b, s // tm),
        in_specs=[
            tok(D_MODEL), tok(ATTN_DIM), tok(SSM_INNER),
            pl.BlockSpec((1, N_MOD, D_MODEL), lambda bi, si: (bi + boff, 0, 0)),
            full(wo_a), full(wo_s), full(g), full(wg), full(wu), full(wd),
        ],
        out_specs=tok(D_MODEL),
        out_shape=jax.ShapeDtypeStruct((b, s, D_MODEL), F32),
        compiler_params=pltpu.CompilerParams(
            dimension_semantics=("parallel", "parallel"), vmem_limit_bytes=VMEM_LIMIT),
        name="out_ffn",
    )(x, attn, ssd, mod3, wo_a, wo_s, g, wg, wu, wd)


def _rope_tables(s):
    t = jnp.arange(s)
    half = HEAD_DIM // 4
    freqs = ROPE_THETA ** (-jnp.arange(half, dtype=F32) / half)
    ang_r = (t // GRID_W).astype(F32)[:, None] * freqs[None, :]
    ang_c = (t % GRID_W).astype(F32)[:, None] * freqs[None, :]
    ang = jnp.concatenate([ang_r, ang_r, ang_c, ang_c], axis=-1)
    first_half = (jnp.arange(HEAD_DIM) % (2 * half)) < half
    cos = jnp.cos(ang)
    sin = jnp.sin(ang)
    sina = jnp.where(first_half, -sin, 0.0)
    sinb = jnp.where(first_half, 0.0, sin)
    rep = LANES // HEAD_DIM
    return tuple(jnp.tile(a, (1, rep)) for a in (cos, sina, sinb))


def _block_mean(width):
    idx = jnp.arange(width) // HEAD_DIM
    return jnp.where(idx[:, None] == idx[None, :], 1.0 / HEAD_DIM, 0.0).astype(BF16)


def _conv_shift_matrix():
    t = jnp.arange(CHUNK)[None, :, None]
    k = jnp.arange(D_CONV)[:, None, None]
    m = jnp.arange(CHUNK + 2 * HALO)[None, None, :]
    sel = (m == t + HALO + k - D_CONV // 2)
    return sel.reshape(D_CONV * CHUNK, CHUNK + 2 * HALO).astype(BF16)


def _pad_lanes(row):
    return jnp.pad(row.astype(F32), (0, LANES - row.shape[0])).reshape(1, LANES)


def kernel(x_prompt, x_sample, c_prompt, c_sample, w_mod, b_mod, norm_mix_g, w_in, q_norm_g, k_norm_g, conv_w, conv_b, dt_bias_fwd, a_log_fwd, dt_bias_bwd, a_log_bwd, d_skip, ssd_norm_g, w_out, norm_ffn_g, w_gate_up, w_down):
    assert w_mod.shape[0] == 1
    l = 0
    c_all = jnp.concatenate([c_prompt, c_sample], axis=0)
    mod3 = _mod(c_all, w_mod[l], b_mod[l]).reshape(c_all.shape[0], N_MOD, D_MODEL)

    w = w_in[l].astype(BF16)
    o0 = 0
    o1 = o0 + ATTN_DIM
    o2 = o1 + KV_DIM
    o3 = o2 + KV_DIM
    o4 = o3 + SSM_INNER
    o5 = o4 + CONV_DIM
    wdt = jnp.pad(w[:, o5:], ((0, 0), (0, LANES - 2 * SSM_HEADS)))
    in_consts = (
        norm_mix_g[l].reshape(1, -1), w[:, o0:o1], w[:, o1:o2], w[:, o2:o3], w[:, o3:o4],
        w[:, o4:o5], wdt,
        jnp.tile(q_norm_g[l], N_Q_HEADS).reshape(1, -1),
        jnp.tile(k_norm_g[l], N_KV_HEADS).reshape(1, -1),
        _block_mean(ATTN_DIM), _block_mean(KV_DIM),
    )
    head_of_lane = jnp.arange(SSM_INNER) // SSM_HEAD_DIM
    expand_f = (jnp.arange(LANES)[:, None] == head_of_lane[None, :]).astype(BF16)
    expand_b = (jnp.arange(LANES)[:, None] == head_of_lane[None, :] + SSM_HEADS).astype(BF16)
    ssd_consts = (
        _conv_shift_matrix(), conv_w[l], conv_b[l].reshape(1, -1),
        _pad_lanes(jnp.concatenate([dt_bias_fwd[l], dt_bias_bwd[l]])),
        _pad_lanes(jnp.concatenate([a_log_fwd[l], a_log_bwd[l]])),
        jnp.concatenate([expand_f, expand_f], axis=0), jnp.concatenate([expand_b, expand_b], axis=0),
        jnp.repeat(d_skip[l], SSM_HEAD_DIM).reshape(1, -1), ssd_norm_g[l].reshape(1, -1),
    )
    wo = w_out[l].astype(BF16)
    wgu = w_gate_up[l].astype(BF16)
    ffn_consts = (wo[:ATTN_DIM], wo[ATTN_DIM:], norm_ffn_g[l].reshape(1, -1),
                  wgu[:, :D_FF], wgu[:, D_FF:], w_down[l].astype(BF16))

    def run(x, boff):
        s = x.shape[1]
        q, kt, vext, z, xbc, dt = _inproj(x, mod3, boff, in_consts + _rope_tables(s))
        attn = _attn(q, kt, vext)
        ssd = _ssd(xbc, dt, z, ssd_consts)
        return _out_ffn(x, attn, ssd, mod3, boff, ffn_consts)

    return run(x_prompt, 0), run(x_sample, c_prompt.shape[0])
```
